```python
import math
import jax, jax.numpy as jnp
from jax import lax
import numpy as np

D_MODEL = 1024
BATCH = 2
SEQ = 8192
DEPTH = 1

N_MEM = 256
XATTN_HEADS = 4
XATTN_HEAD_DIM = D_MODEL // XATTN_HEADS

D_MIX = D_MODEL
D_GLA = D_MIX // 2
D_LRU = D_MIX - D_GLA

GLA_HEADS = 4
GLA_DK_TOTAL = D_GLA // 2
GLA_DK = GLA_DK_TOTAL // GLA_HEADS
GLA_DV = D_GLA // GLA_HEADS
GLA_LOWRANK = 16
GLA_GATE_NORMALIZER = 16.0
GLA_CHUNK = 64

LRU_HEADS = 8
LRU_BLOCK = D_LRU // LRU_HEADS
LRU_CONV_W = 4
LRU_C = 8.0

D_FF = 2816
FFN_CONV_W = 3

EPS = 1e-6

IN_SPLITS = (GLA_DK_TOTAL, GLA_DK_TOTAL, D_GLA, D_GLA, GLA_LOWRANK, D_LRU, D_LRU)
D_IN = sum(IN_SPLITS)

kernel_name = "hybrid_gla_rglru_xattn_convffn"


def _rmsnorm(x, g):
    xf = x.astype(jnp.float32)
    y = xf * lax.rsqrt(jnp.mean(xf * xf, axis=-1, keepdims=True) + EPS)
    return (y * g.astype(jnp.float32)).astype(x.dtype)


def _causal_dwconv(x, w, b):
    width, ch = w.shape
    y = lax.conv_general_dilated(
        x, w[:, None, :].astype(x.dtype), window_strides=(1,), padding=[(width - 1, 0)],
        dimension_numbers=("NWC", "WIO", "NWC"), feature_group_count=ch)
    return y + b.astype(x.dtype)


def _gla_chunk_step(state, inp):
    q, k, v, g = inp
    c = q.shape[2]
    b = jnp.cumsum(g, axis=2)
    inter = jnp.einsum('bhcd,bhde->bhce', q * jnp.exp(b), state)
    diff = b[:, :, :, None, :] - b[:, :, None, :, :]
    causal = jnp.tril(jnp.ones((c, c), dtype=bool))[:, :, None]
    decay = jnp.exp(jnp.where(causal, diff, -jnp.inf))
    scores = jnp.einsum('bhid,bhjd,bhijd->bhij', q, k, decay)
    intra = jnp.einsum('bhij,bhje->bhie', scores, v)
    b_last = b[:, :, -1, :]
    k_dec = k * jnp.exp(b_last[:, :, None, :] - b)
    state = jnp.exp(b_last)[..., None] * state + jnp.einsum('bhjd,bhje->bhde', k_dec, v)
    return state, inter + intra


def _gla(q, k, v, glog):
    bsz, s, h, dk = q.shape
    dv = v.shape[-1]
    n = s // GLA_CHUNK

    def to_chunks(t):
        return t.astype(jnp.float32).reshape(bsz, n, GLA_CHUNK, h, t.shape[-1]).transpose(1, 0, 3, 2, 4)

    state0 = jnp.zeros((bsz, h, dk, dv), jnp.float32)
    _, ys = lax.scan(_gla_chunk_step, state0,
                     (to_chunks(q), to_chunks(k), to_chunks(v), to_chunks(glog)))
    return ys.transpose(1, 0, 3, 2, 4).reshape(bsz, s, h, dv)


def _rglru(xc, w_a, b_a, w_x, b_x, lam):
    bsz, s, _ = xc.shape
    xf = xc.astype(jnp.float32)
    xb = xf.reshape(bsz, s, LRU_HEADS, LRU_BLOCK)
    r = jax.nn.sigmoid(jnp.einsum('bshi,hij->bshj', xb, w_a.astype(jnp.float32)).reshape(bsz, s, D_LRU) + b_a)
    i = jax.nn.sigmoid(jnp.einsum('bshi,hij->bshj', xb, w_x.astype(jnp.float32)).reshape(bsz, s, D_LRU) + b_x)
    log_a = -LRU_C * r * jax.nn.softplus(-lam.astype(jnp.float32))
    a = jnp.exp(log_a)
    mult = jnp.sqrt(-jnp.expm1(2.0 * log_a))
    mult = jnp.where(jnp.arange(s)[None, :, None] == 0, 1.0, mult)
    bterm = mult * (i * xf)

    def combine(lhs, rhs):
        a1, b1 = lhs
        a2, b2 = rhs
        return a1 * a2, a2 * b1 + b2

    _, hseq = lax.associative_scan(combine, (a, bterm), axis=1)
    return hseq


def setup_inputs(seed: int = 0) -> dict:
    key = jax.random.key(seed)
    ks = jax.random.split(key, 32)
    L = DEPTH

    def nrm(k, shape, fan_in):
        return jax.random.normal(k, shape, jnp.float32) * (fan_in ** -0.5)

    def gain(k, shape):
        return 1.0 + 0.02 * jax.random.normal(k, shape, jnp.float32)

    def small(k, shape):
        return 0.01 * jax.random.normal(k, shape, jnp.float32)

    a0 = jax.random.uniform(ks[10], (L, D_LRU), jnp.float32, 0.9, 0.999)
    lru_lambda = jnp.log(a0) - jnp.log1p(-a0)

    return {
        "x": jax.random.normal(ks[0], (BATCH, SEQ, D_MODEL), jnp.float32),
        "mem": jax.random.normal(ks[1], (BATCH, N_MEM, D_MODEL), jnp.float32),
        "g_mix": gain(ks[2], (L, D_MODEL)),
        "w_in": nrm(ks[3], (L, D_MODEL, D_IN), D_MODEL),
        "w_gk": nrm(ks[4], (L, GLA_LOWRANK, GLA_DK_TOTAL), GLA_LOWRANK),
        "b_gk": small(ks[5], (L, GLA_DK_TOTAL)),
        "g_gla_head": gain(ks[6], (L, GLA_DV)),
        "conv_lru_w": nrm(ks[7], (L, LRU_CONV_W, D_LRU), LRU_CONV_W),
        "conv_lru_b": small(ks[8], (L, D_LRU)),
        "w_rg_a": nrm(ks[9], (L, LRU_HEADS, LRU_BLOCK, LRU_BLOCK), LRU_BLOCK),
        "b_rg_a": small(ks[11], (L, D_LRU)),
        "w_rg_x": nrm(ks[12], (L, LRU_HEADS, LRU_BLOCK, LRU_BLOCK), LRU_BLOCK),
        "b_rg_x": small(ks[13], (L, D_LRU)),
        "lru_lambda": lru_lambda,
        "w_mix_out": nrm(ks[14], (L, D_MIX, D_MODEL), D_MIX),
        "g_xattn": gain(ks[15], (L, D_MODEL)),
        "g_mem": gain(ks[16], (L, D_MODEL)),
        "w_xq": nrm(ks[17], (L, D_MODEL, D_MODEL), D_MODEL),
        "w_xkv": nrm(ks[18], (L, D_MODEL, 2 * D_MODEL), D_MODEL),
        "w_xo": nrm(ks[19], (L, D_MODEL, D_MODEL), D_MODEL),
        "g_ffn": gain(ks[20], (L, D_MODEL)),
        "w_ffn_in": nrm(ks[21], (L, D_MODEL, 2 * D_FF), D_MODEL),
        "conv_ffn_w": nrm(ks[22], (L, FFN_CONV_W, D_FF), FFN_CONV_W),
        "conv_ffn_b": small(ks[23], (L, D_FF)),
        "w_ffn_out": nrm(ks[24], (L, D_FF, D_MODEL), D_FF),
        "g_final": gain(ks[25], (D_MODEL,)),
    }


def reference(x, mem, g_mix, w_in, w_gk, b_gk, g_gla_head, conv_lru_w, conv_lru_b,
              w_rg_a, b_rg_a, w_rg_x, b_rg_x, lru_lambda, w_mix_out,
              g_xattn, g_mem, w_xq, w_xkv, w_xo,
              g_ffn, w_ffn_in, conv_ffn_w, conv_ffn_b, w_ffn_out, g_final):
    bsz, s, d = x.shape
    offsets = np.cumsum(IN_SPLITS)[:-1].tolist()

    for l in range(DEPTH):
        h = _rmsnorm(x, g_mix[l])
        u = h @ w_in[l]
        q, k, v, og, gk_lr, xr, xg = jnp.split(u, offsets, axis=-1)

        glog = jax.nn.log_sigmoid((gk_lr.astype(jnp.float32) @ w_gk[l].astype(jnp.float32)) + b_gk[l]) / GLA_GATE_NORMALIZER
        qh = q.reshape(bsz, s, GLA_HEADS, GLA_DK) * (GLA_DK ** -0.5)
        kh = k.reshape(bsz, s, GLA_HEADS, GLA_DK)
        vh = v.reshape(bsz, s, GLA_HEADS, GLA_DV)
        gh = glog.reshape(bsz, s, GLA_HEADS, GLA_DK)
        o = _gla(qh, kh, vh, gh)
        o = o * lax.rsqrt(jnp.mean(o * o, axis=-1, keepdims=True) + EPS) * g_gla_head[l].astype(jnp.float32)
        y_gla = o.reshape(bsz, s, D_GLA) * jax.nn.silu(og.astype(jnp.float32))

        xc = _causal_dwconv(xr, conv_lru_w[l], conv_lru_b[l])
        hr = _rglru(xc, w_rg_a[l], b_rg_a[l], w_rg_x[l], b_rg_x[l], lru_lambda[l])
        y_lru = hr * jax.nn.gelu(xg.astype(jnp.float32), approximate=True)

        y_mix = jnp.concatenate([y_gla, y_lru], axis=-1).astype(x.dtype)
        x = x + y_mix @ w_mix_out[l]

        hq = _rmsnorm(x, g_xattn[l])
        hm = _rmsnorm(mem, g_mem[l])
        xq = (hq @ w_xq[l]).reshape(bsz, s, XATTN_HEADS, XATTN_HEAD_DIM)
        xk, xv = jnp.split(hm @ w_xkv[l], 2, axis=-1)
        xk = xk.reshape(bsz, N_MEM, XATTN_HEADS, XATTN_HEAD_DIM)
        xv = xv.reshape(bsz, N_MEM, XATTN_HEADS, XATTN_HEAD_DIM)
        sc = jnp.einsum('bshd,bmhd->bhsm', xq.astype(jnp.float32), xk.astype(jnp.float32)) * (XATTN_HEAD_DIM ** -0.5)
        p = jax.nn.softmax(sc, axis=-1)
        att = jnp.einsum('bhsm,bmhd->bshd', p, xv.astype(jnp.float32)).reshape(bsz, s, D_MODEL)
        x = x + att.astype(x.dtype) @ w_xo[l]

        hf = _rmsnorm(x, g_ffn[l])
        gate, up = jnp.split(hf @ w_ffn_in[l], 2, axis=-1)
        gate = _causal_dwconv(gate, conv_ffn_w[l], conv_ffn_b[l])
        f = (jax.nn.silu(gate.astype(jnp.float32)) * up.astype(jnp.float32)).astype(x.dtype)
        x = x + f @ w_ffn_out[l]

    return _rmsnorm(x, g_final)
```

```python
import functools
import math

import jax
import jax.numpy as jnp
from jax import lax
from jax.experimental import pallas as pl
from jax.experimental.pallas import tpu as pltpu

F32 = jnp.float32
BF16 = jnp.bfloat16

D_MODEL = 1024
N_MEM = 256
XATTN_HEADS = 4
XATTN_HEAD_DIM = D_MODEL // XATTN_HEADS
D_GLA = 512
D_LRU = 512
GLA_HEADS = 4
GLA_DK_TOTAL = 256
GLA_DK = 64
GLA_DV = 128
GLA_LOWRANK = 16
GLA_GATE_NORMALIZER = 16.0
GLA_CHUNK = 64
LRU_HEADS = 8
LRU_BLOCK = 64
LRU_CONV_W = 4
LRU_C = 8.0
D_FF = 2816
FFN_CONV_W = 3
EPS = 1e-6

LANES = 128
SUBLANES = 8
VMEM_LIMIT_BYTES = 56 * 1024 * 1024

TM_IN = 512
TT_MIX = 256
TM_TAIL = 256
FF_CHUNK = 256
N_FF_CHUNKS = D_FF // FF_CHUNK
assert N_FF_CHUNKS * FF_CHUNK == D_FF


def _rmsnorm(x, g):
    return x * lax.rsqrt(jnp.mean(x * x, axis=-1, keepdims=True) + EPS) * g


def _softplus(x):
    return jnp.maximum(x, 0.0) + jnp.log1p(jnp.exp(-jnp.abs(x)))


def _sigmoid(x):
    return 1.0 / (1.0 + jnp.exp(-x))


def _silu(x):
    return x * _sigmoid(x)


def _gelu_tanh(x):
    c = math.sqrt(2.0 / math.pi)
    return 0.5 * x * (1.0 + jnp.tanh(c * (x + 0.044715 * (x * x * x))))


def _dot(a, b):
    return jnp.dot(a, b, preferred_element_type=F32)


def _dot_nt(a, b):
    return lax.dot_general(a, b, (((1,), (1,)), ((), ())), preferred_element_type=F32)


def _dot_tn(a, b):
    return lax.dot_general(a, b, (((0,), (0,)), ((), ())), preferred_element_type=F32)


def _memory_kv_kernel(mem_ref, g_ref, wk_ref, wv_ref, k_ref, v_ref):
    hm = _rmsnorm(mem_ref[0], g_ref[...]).astype(BF16)
    k_ref[0] = _dot(hm, wk_ref[...]).astype(BF16)
    v_ref[0] = _dot(hm, wv_ref[...]).astype(BF16)


def _memory_kv(mem, g_mem, wk, wv):
    bsz = mem.shape[0]
    const = lambda b: (0, 0)
    return pl.pallas_call(
        _memory_kv_kernel,
        grid=(bsz,),
        in_specs=[
            pl.BlockSpec((1, N_MEM, D_MODEL), lambda b: (b, 0, 0)),
            pl.BlockSpec((1, D_MODEL), const),
            pl.BlockSpec((D_MODEL, D_MODEL), const),
            pl.BlockSpec((D_MODEL, D_MODEL), const),
        ],
        out_specs=[
            pl.BlockSpec((1, N_MEM, D_MODEL), lambda b: (b, 0, 0)),
            pl.BlockSpec((1, N_MEM, D_MODEL), lambda b: (b, 0, 0)),
        ],
        out_shape=[jax.ShapeDtypeStruct((bsz, N_MEM, D_MODEL), BF16)] * 2,
        compiler_params=pltpu.CompilerParams(
            dimension_semantics=("arbitrary",), vmem_limit_bytes=VMEM_LIMIT_BYTES),
        name="memory_kv",
    )(mem, g_mem, wk, wv)


def _in_proj_kernel(x_ref, g_ref, wq_ref, wk_ref, wv_ref, wog_ref, wlr_ref, wxr_ref, wxg_ref,
                    wgk_ref, bgk_ref,
                    q_ref, k_ref, v_ref, og_ref, glog_ref, xr_ref, xg_ref):
    h = _rmsnorm(x_ref[...], g_ref[...]).astype(BF16)
    q_ref[...] = _dot(h, wq_ref[...]).astype(BF16)
    k_ref[...] = _dot(h, wk_ref[...]).astype(BF16)
    v_ref[...] = _dot(h, wv_ref[...]).astype(BF16)
    og_ref[...] = _dot(h, wog_ref[...]).astype(BF16)
    xr_ref[...] = _dot(h, wxr_ref[...]).astype(BF16)
    xg_ref[...] = _dot(h, wxg_ref[...]).astype(BF16)
    lr = _dot(h, wlr_ref[...])
    z = _dot(lr.astype(BF16), wgk_ref[...]) + bgk_ref[...]
    glog_ref[...] = -_softplus(-z) * (1.0 / GLA_GATE_NORMALIZER)


def _in_proj(x2d, g_mix, wq, wk, wv, wog, wlr, wxr, wxg, wgk, bgk):
    n_tok = x2d.shape[0]
    tm = TM_IN
    const = lambda i: (0, 0)
    row = lambda i: (i, 0)
    w_spec = lambda w: pl.BlockSpec(w.shape, const)
    out_w = (GLA_DK_TOTAL, GLA_DK_TOTAL, D_GLA, D_GLA, GLA_DK_TOTAL, D_LRU, D_LRU)
    out_dt = (BF16, BF16, BF16, BF16, F32, BF16, BF16)
    return pl.pallas_call(
        _in_proj_kernel,
        grid=(n_tok // tm,),
        in_specs=[pl.BlockSpec((tm, D_MODEL), row), pl.BlockSpec((1, D_MODEL), const)]
        + [w_spec(w) for w in (wq, wk, wv, wog, wlr, wxr, wxg, wgk)]
        + [pl.BlockSpec((1, GLA_DK_TOTAL), const)],
        out_specs=[pl.BlockSpec((tm, w), row) for w in out_w],
        out_shape=[jax.ShapeDtypeStruct((n_tok, w), dt) for w, dt in zip(out_w, out_dt)],
        compiler_params=pltpu.CompilerParams(
            dimension_semantics=("arbitrary",), vmem_limit_bytes=VMEM_LIMIT_BYTES),
        name="in_proj",
    )(x2d, g_mix, wq, wk, wv, wog, wlr, wxr, wxg, wgk, bgk)


def _gla_gather_rows_back(b, p, s):
    x = b
    bit = 0
    while (1 << bit) < s:
        sh = 1 << bit
        cond = jnp.logical_and(p >= s, ((p - s) >> bit) & 1 == 1)
        x = jnp.where(cond, pltpu.roll(x, sh, 0), x)
        bit += 1
    return x


def _gla_gather_rows_fwd(b, p, s, n_rows):
    x = b
    u = s - p
    bit = 0
    while (1 << bit) <= s:
        sh = 1 << bit
        cond = jnp.logical_and(p < s, (u >> bit) & 1 == 1)
        x = jnp.where(cond, pltpu.roll(x, n_rows - sh, 0), x)
        bit += 1
    return x


def _gla_chunk(qc, kc, gc, vc, st_ref, consts):
    tril, row, lane_head, head_masks, level_masks, eseg, bd_mask = consts
    c = GLA_CHUNK
    g_hi = gc.astype(BF16)
    g_lo = (gc - g_hi.astype(F32)).astype(BF16)
    b = _dot(tril, g_hi) + _dot(tril, g_lo)
    b_last = b[c - 1:c, :]
    scale = GLA_DK ** -0.5

    st = st_ref[...]
    q_in = (qc * jnp.exp(b) * scale).astype(BF16)
    o = _dot_nt(q_in, st.astype(BF16))

    o = o + _dot((qc * kc * scale).astype(BF16), eseg) * vc.astype(F32)

    a_sc = jnp.zeros((GLA_HEADS * c, c), F32)
    for li, s in enumerate((32, 16, 8, 4, 2, 1)):
        p = row & (2 * s - 1)
        dq = b - _gla_gather_rows_back(b, p, s)
        dk = jnp.where(p < s, _gla_gather_rows_fwd(b, p, s, c) - b, 0.0)
        qt = qc * jnp.exp(dq) * scale
        kt = (kc * jnp.exp(dk)).astype(BF16)
        q_exp = jnp.concatenate(
            [jnp.where(head_masks[h], qt, 0.0) for h in range(GLA_HEADS)], axis=0).astype(BF16)
        sc = _dot_nt(q_exp, kt)
        a_sc = a_sc + jnp.where(level_masks[li], sc, 0.0)
    intra = [
        _dot(a_sc[h * c:(h + 1) * c, :].astype(BF16), vc[:, h * GLA_DV:(h + 1) * GLA_DV])
        for h in range(GLA_HEADS)
    ]
    o = o + jnp.concatenate(intra, axis=1)

    k_dec = (kc * jnp.exp(b_last - b)).astype(BF16)
    kv_t = _dot_tn(vc, k_dec)
    st_ref[...] = st * jnp.exp(b_last) + jnp.where(bd_mask, kv_t, 0.0)
    return o


def _mixers_kernel(q_ref, k_ref, v_ref, og_ref, glog_ref, xr_ref, xg_ref,
                   ghead_ref, cw_ref, cb_ref, wa_ref, ba_ref, wx_ref, bx_ref, lam_ref,
                   y_ref,
                   st_sc, o_sc, xbuf_sc, hcar_sc):
    tt = TT_MIX
    c = GLA_CHUNK
    t_idx = pl.program_id(1)

    @pl.when(t_idx == 0)
    def _():
        st_sc[...] = jnp.zeros_like(st_sc)
        xbuf_sc[0:SUBLANES, :] = jnp.zeros((SUBLANES, D_LRU), F32)
        hcar_sc[...] = jnp.zeros_like(hcar_sc)

    row = lax.broadcasted_iota(jnp.int32, (c, GLA_DK_TOTAL), 0)
    lane = lax.broadcasted_iota(jnp.int32, (c, GLA_DK_TOTAL), 1)
    lane_head = lane // GLA_DK
    head_masks = [lane_head == h for h in range(GLA_HEADS)]
    tri_r = lax.broadcasted_iota(jnp.int32, (c, c), 0)
    tri_c = lax.broadcasted_iota(jnp.int32, (c, c), 1)
    tril = (tri_r >= tri_c).astype(BF16)
    li_r = lax.broadcasted_iota(jnp.int32, (GLA_HEADS * c, c), 0) & (c - 1)
    li_c = lax.broadcasted_iota(jnp.int32, (GLA_HEADS * c, c), 1)
    level_masks = []
    for s in (32, 16, 8, 4, 2, 1):
        m2 = 2 * s
        same_blk = (li_r // m2) == (li_c // m2)
        level_masks.append(
            jnp.logical_and(same_blk, jnp.logical_and((li_r & (m2 - 1)) >= s, (li_c & (m2 - 1)) < s)))
    es_r = lax.broadcasted_iota(jnp.int32, (GLA_DK_TOTAL, D_GLA), 0) // GLA_DK
    es_c = lax.broadcasted_iota(jnp.int32, (GLA_DK_TOTAL, D_GLA), 1) // GLA_DV
    eseg = (es_r == es_c).astype(BF16)
    bd_r = lax.broadcasted_iota(jnp.int32, (D_GLA, GLA_DK_TOTAL), 0) // GLA_DV
    bd_c = lax.broadcasted_iota(jnp.int32, (D_GLA, GLA_DK_TOTAL), 1) // GLA_DK
    bd_mask = bd_r == bd_c
    consts = (tril, row, lane_head, head_masks, level_masks, eseg, bd_mask)

    def chunk_body(ci, carry):
        r0 = pl.multiple_of(ci * c, c)
        qc = q_ref[0, pl.ds(r0, c), :].astype(F32)
        kc = k_ref[0, pl.ds(r0, c), :].astype(F32)
        gc = glog_ref[0, pl.ds(r0, c), :]
        vc = v_ref[0, pl.ds(r0, c), :]
        o_sc[pl.ds(r0, c), :] = _gla_chunk(qc, kc, gc, vc, st_sc, consts)
        return carry

    lax.fori_loop(0, tt // c, chunk_body, 0)

    o = o_sc[...]
    og = og_ref[0].astype(F32)
    ghead = ghead_ref[...]
    for h in range(GLA_HEADS):
        sl = slice(h * GLA_DV, (h + 1) * GLA_DV)
        y_ref[0, :, sl] = (_rmsnorm(o[:, sl], ghead) * _silu(og[:, sl])).astype(BF16)

    xr = xr_ref[0].astype(F32)
    xbuf_sc[SUBLANES:SUBLANES + tt, :] = xr
    cw = cw_ref[...]
    xc = (cw[3:4, :] * xr
          + cw[2:3, :] * xbuf_sc[SUBLANES - 1:SUBLANES - 1 + tt, :]
          + cw[1:2, :] * xbuf_sc[SUBLANES - 2:SUBLANES - 2 + tt, :]
          + cw[0:1, :] * xbuf_sc[SUBLANES - 3:SUBLANES - 3 + tt, :]
          + cb_ref[...])
    xbuf_sc[0:SUBLANES, :] = xr[tt - SUBLANES:tt, :]

    xcb = xc.astype(BF16)
    r_gate = _sigmoid(_dot(xcb, wa_ref[...]) + ba_ref[...])
    i_gate = _sigmoid(_dot(xcb, wx_ref[...]) + bx_ref[...])
    log_a = (-LRU_C) * r_gate * _softplus(-lam_ref[...])
    a = jnp.exp(log_a)
    th = jnp.tanh(log_a)
    mult = jnp.sqrt(-2.0 * th / (1.0 - th))
    trow = lax.broadcasted_iota(jnp.int32, (tt, D_LRU), 0)
    mult = jnp.where(jnp.logical_and(trow == 0, t_idx == 0), 1.0, mult)
    bterm = mult * (i_gate * xc)

    sh = 1
    while sh < tt:
        keep = trow >= sh
        a_s = jnp.where(keep, pltpu.roll(a, sh, 0), 1.0)
        b_s = jnp.where(keep, pltpu.roll(bterm, sh, 0), 0.0)
        bterm = a * b_s + bterm
        a = a * a_s
        sh *= 2
    hseq = bterm + a * hcar_sc[0:1, :]
    hcar_sc[0:1, :] = hseq[tt - 1:tt, :]

    y_ref[0, :, D_GLA:] = (hseq * _gelu_tanh(xg_ref[0].astype(F32))).astype(BF16)


def _mixers(q, k, v, og, glog, xr, xg, ghead, cw, cb, wa_bd, ba, wx_bd, bx, lam):
    bsz, seq, _ = q.shape
    tt = TT_MIX
    tok = lambda w: pl.BlockSpec((1, tt, w), lambda b, t: (b, t, 0))
    const = lambda shp: pl.BlockSpec(shp, lambda b, t: (0, 0))
    return pl.pallas_call(
        _mixers_kernel,
        grid=(bsz, seq // tt),
        in_specs=[tok(GLA_DK_TOTAL), tok(GLA_DK_TOTAL), tok(D_GLA), tok(D_GLA), tok(GLA_DK_TOTAL),
                  tok(D_LRU), tok(D_LRU),
                  const((1, GLA_DV)), const((LRU_CONV_W, D_LRU)), const((1, D_LRU)),
                  const((D_LRU, D_LRU)), const((1, D_LRU)), const((D_LRU, D_LRU)), const((1, D_LRU)),
                  const((1, D_LRU))],
        out_specs=pl.BlockSpec((1, tt, D_MODEL), lambda b, t: (b, t, 0)),
        out_shape=jax.ShapeDtypeStruct((bsz, seq, D_MODEL), BF16),
        scratch_shapes=[
            pltpu.VMEM((D_GLA, GLA_DK_TOTAL), F32),
            pltpu.VMEM((tt, D_GLA), F32),
            pltpu.VMEM((tt + SUBLANES, D_LRU), F32),
            pltpu.VMEM((SUBLANES, D_LRU), F32),
        ],
        compiler_params=pltpu.CompilerParams(
            dimension_semantics=("arbitrary", "arbitrary"), vmem_limit_bytes=VMEM_LIMIT_BYTES),
        name="mixers",
    )(q, k, v, og, glog, xr, xg, ghead, cw, cb, wa_bd, ba, wx_bd, bx, lam)


def _tail_kernel(x_ref, y_ref, km_ref, vm_ref,
                 wmo_ref, gxa_ref, wxq_ref, wxo_ref,
                 gffn_ref, wg_ref, wu_ref, cwb_ref, wo_ref, gfin_ref,
                 out_ref,
                 halo_sc, gbuf_sc, acc_sc):
    tm = TM_TAIL
    t_idx = pl.program_id(1)

    @pl.when(t_idx == 0)
    def _():
        halo_sc[...] = jnp.zeros_like(halo_sc)

    x1 = x_ref[0] + _dot(y_ref[0], wmo_ref[...])

    hq = _rmsnorm(x1, gxa_ref[...]).astype(BF16)
    xq = _dot(hq, wxq_ref[...])
    att = []
    sm_scale = XATTN_HEAD_DIM ** -0.5
    for h in range(XATTN_HEADS):
        sl = slice(h * XATTN_HEAD_DIM, (h + 1) * XATTN_HEAD_DIM)
        s = _dot_nt(xq[:, sl].astype(BF16), km_ref[0, :, sl]) * sm_scale
        e = jnp.exp(s - jnp.max(s, axis=-1, keepdims=True))
        p = e * (1.0 / jnp.sum(e, axis=-1, keepdims=True))
        att.append(_dot(p.astype(BF16), vm_ref[0, :, sl]))
    att = jnp.concatenate(att, axis=1).astype(BF16)
    x2 = x1 + _dot(att, wxo_ref[...])

    hf = _rmsnorm(x2, gffn_ref[...]).astype(BF16)
    acc_sc[...] = x2

    def ff_body(ci, carry):
        gate = _dot(hf, wg_ref[ci])
        up = _dot(hf, wu_ref[ci])
        cwb = cwb_ref[ci]
        gbuf_sc[0:SUBLANES, :] = halo_sc[ci]
        gbuf_sc[SUBLANES:SUBLANES + tm, :] = gate
        conv = (cwb[2:3, :] * gate
                + cwb[1:2, :] * gbuf_sc[SUBLANES - 1:SUBLANES - 1 + tm, :]
                + cwb[0:1, :] * gbuf_sc[SUBLANES - 2:SUBLANES - 2 + tm, :]
                + cwb[3:4, :])
        halo_sc[ci] = gate[tm - SUBLANES:tm, :]
        f = (_silu(conv) * up).astype(BF16)
        acc_sc[...] += _dot(f, wo_ref[ci])
        return carry

    lax.fori_loop(0, N_FF_CHUNKS, ff_body, 0)

    out_ref[0] = _rmsnorm(acc_sc[...], gfin_ref[...])


def _tail(x, y_mix, k_mem, v_mem, wmo, gxa, wxq, wxo, gffn, wg3, wu3, cwb3, wo3, gfin):
    bsz, seq, _ = x.shape
    tm = TM_TAIL
    tok = lambda: pl.BlockSpec((1, tm, D_MODEL), lambda b, t: (b, t, 0))
    mem = lambda: pl.BlockSpec((1, N_MEM, D_MODEL), lambda b, t: (b, 0, 0))
    single = pl.Buffered(1)
    c2 = lambda shp: pl.BlockSpec(shp, lambda b, t: (0, 0), pipeline_mode=single)
    c3 = lambda shp: pl.BlockSpec(shp, lambda b, t: (0, 0, 0), pipeline_mode=single)
    return pl.pallas_call(
        _tail_kernel,
        grid=(bsz, seq // tm),
        in_specs=[tok(), tok(), mem(), mem(),
                  c2((D_MODEL, D_MODEL)), c2((1, D_MODEL)), c2((D_MODEL, D_MODEL)), c2((D_MODEL, D_MODEL)),
                  c2((1, D_MODEL)),
                  c3((N_FF_CHUNKS, D_MODEL, FF_CHUNK)), c3((N_FF_CHUNKS, D_MODEL, FF_CHUNK)),
                  c3((N_FF_CHUNKS, SUBLANES, FF_CHUNK)), c3((N_FF_CHUNKS, FF_CHUNK, D_MODEL)),
                  c2((1, D_MODEL))],
        out_specs=tok(),
        out_shape=jax.ShapeDtypeStruct((bsz, seq, D_MODEL), F32),
        scratch_shapes=[
            pltpu.VMEM((N_FF_CHUNKS, SUBLANES, FF_CHUNK), F32),
            pltpu.VMEM((tm + SUBLANES, FF_CHUNK), F32),
            pltpu.VMEM((tm, D_MODEL), F32),
        ],
        compiler_params=pltpu.CompilerParams(
            dimension_semantics=("arbitrary", "arbitrary"), vmem_limit_bytes=VMEM_LIMIT_BYTES),
        name="tail",
    )(x, y_mix, k_mem, v_mem, wmo, gxa, wxq, wxo, gffn, wg3, wu3, cwb3, wo3, gfin)


def _block_diag(w):
    h, n, _ = w.shape
    eye = jnp.eye(h, dtype=w.dtype)
    return (eye[:, None, :, None] * w[:, :, None, :]).reshape(h * n, h * n)


def kernel(x, mem, g_mix, w_in, w_gk, b_gk, g_gla_head, conv_lru_w, conv_lru_b, w_rg_a, b_rg_a, w_rg_x, b_rg_x, lru_lambda, w_mix_out, g_xattn, g_mem, w_xq, w_xkv, w_xo, g_ffn, w_ffn_in, conv_ffn_w, conv_ffn_b, w_ffn_out, g_final):
    bsz, seq, d = x.shape
    l = 0

    wi = w_in[l].astype(BF16)
    o_q, o_k, o_v, o_og, o_lr, o_xr, o_xg = 0, 256, 512, 1024, 1536, 1552, 2064
    wq, wk, wv, wog = wi[:, o_q:o_k], wi[:, o_k:o_v], wi[:, o_v:o_og], wi[:, o_og:o_lr]
    wlr = jnp.pad(wi[:, o_lr:o_xr], ((0, 0), (0, LANES - GLA_LOWRANK)))
    wxr, wxg = wi[:, o_xr:o_xg], wi[:, o_xg:]
    wgk = jnp.pad(w_gk[l].astype(BF16), ((0, LANES - GLA_LOWRANK), (0, 0)))
    row2 = lambda v: v.reshape(1, -1).astype(F32)

    wkv = w_xkv[l].astype(BF16)
    k_mem, v_mem = _memory_kv(mem, row2(g_mem[l]), wkv[:, :D_MODEL], wkv[:, D_MODEL:])

    q, k, v, og, glog, xr, xg = _in_proj(
        x.reshape(bsz * seq, d), row2(g_mix[l]), wq, wk, wv, wog, wlr, wxr, wxg, wgk, row2(b_gk[l]))
    r3 = lambda t: t.reshape(bsz, seq, t.shape[-1])

    y_mix = _mixers(
        r3(q), r3(k), r3(v), r3(og), r3(glog), r3(xr), r3(xg),
        row2(g_gla_head[l]), conv_lru_w[l].astype(F32), row2(conv_lru_b[l]),
        _block_diag(w_rg_a[l]).astype(BF16), row2(b_rg_a[l]),
        _block_diag(w_rg_x[l]).astype(BF16), row2(b_rg_x[l]), row2(lru_lambda[l]))

    wff = w_ffn_in[l].astype(BF16)
    wg3 = wff[:, :D_FF].reshape(D_MODEL, N_FF_CHUNKS, FF_CHUNK).transpose(1, 0, 2)
    wu3 = wff[:, D_FF:].reshape(D_MODEL, N_FF_CHUNKS, FF_CHUNK).transpose(1, 0, 2)
    cwb = jnp.concatenate(
        [conv_ffn_w[l].astype(F32), conv_ffn_b[l].reshape(1, D_FF).astype(F32),
         jnp.zeros((SUBLANES - FFN_CONV_W - 1, D_FF), F32)], axis=0)
    cwb3 = cwb.reshape(SUBLANES, N_FF_CHUNKS, FF_CHUNK).transpose(1, 0, 2)
    wo3 = w_ffn_out[l].astype(BF16).reshape(N_FF_CHUNKS, FF_CHUNK, D_MODEL)

    return _tail(
        x, y_mix, k_mem, v_mem,
        w_mix_out[l].astype(BF16), row2(g_xattn[l]), w_xq[l].astype(BF16), w_xo[l].astype(BF16),
        row2(g_ffn[l]), wg3, wu3, cwb3, wo3, row2(g_final))
```

```python
import functools
import math

import jax
import jax.numpy as jnp
from jax import lax
from jax.experimental import pallas as pl
from jax.experimental.pallas import tpu as pltpu

F32 = jnp.float32
BF16 = jnp.bfloat16

D_MODEL = 1024
N_MEM = 256
XATTN_HEADS = 4
XATTN_HEAD_DIM = D_MODEL // XATTN_HEADS
D_GLA = 512
D_LRU = 512
GLA_HEADS = 4
GLA_DK_TOTAL = 256
GLA_DK = 64
GLA_DV = 128
GLA_LOWRANK = 16
GLA_GATE_NORMALIZER = 16.0
GLA_CHUNK = 64
LRU_HEADS = 8
LRU_BLOCK = 64
LRU_CONV_W = 4
LRU_C = 8.0
D_FF = 2816
FFN_CONV_W = 3
EPS = 1e-6

LANES = 128
SUBLANES = 8
VMEM_LIMIT_BYTES = 56 * 1024 * 1024

TM_IN = 512
TT_MIX = 256
TM_TAIL = 512
TAIL_ROW_GROUPS = 2
FF_CHUNK = 256
N_FF_CHUNKS = D_FF // FF_CHUNK
assert N_FF_CHUNKS * FF_CHUNK == D_FF


def _rmsnorm(x, g):
    return x * lax.rsqrt(jnp.mean(x * x, axis=-1, keepdims=True) + EPS) * g


def _softplus(x):
    return jnp.maximum(x, 0.0) + jnp.log1p(jnp.exp(-jnp.abs(x)))


def _sigmoid(x):
    return 1.0 / (1.0 + jnp.exp(-x))


def _silu(x):
    return x * _sigmoid(x)


def _gelu_tanh(x):
    c = math.sqrt(2.0 / math.pi)
    return 0.5 * x * (1.0 + jnp.tanh(c * (x + 0.044715 * (x * x * x))))


def _dot(a, b):
    return jnp.dot(a, b, preferred_element_type=F32)


def _dot_nt(a, b):
    return lax.dot_general(a, b, (((1,), (1,)), ((), ())), preferred_element_type=F32)


def _dot_tn(a, b):
    return lax.dot_general(a, b, (((0,), (0,)), ((), ())), preferred_element_type=F32)


def _memory_kv_kernel(mem_ref, g_ref, wk_ref, wv_ref, k_ref, v_ref):
    hm = _rmsnorm(mem_ref[0], g_ref[...]).astype(BF16)
    k_ref[0] = _dot(hm, wk_ref[...]).astype(BF16)
    v_ref[0] = _dot(hm, wv_ref[...]).astype(BF16)


def _memory_kv(mem, g_mem, wk, wv):
    bsz = mem.shape[0]
    const = lambda b: (0, 0)
    return pl.pallas_call(
        _memory_kv_kernel,
        grid=(bsz,),
        in_specs=[
            pl.BlockSpec((1, N_MEM, D_MODEL), lambda b: (b, 0, 0)),
            pl.BlockSpec((1, D_MODEL), const),
            pl.BlockSpec((D_MODEL, D_MODEL), const),
            pl.BlockSpec((D_MODEL, D_MODEL), const),
        ],
        out_specs=[
            pl.BlockSpec((1, N_MEM, D_MODEL), lambda b: (b, 0, 0)),
            pl.BlockSpec((1, N_MEM, D_MODEL), lambda b: (b, 0, 0)),
        ],
        out_shape=[jax.ShapeDtypeStruct((bsz, N_MEM, D_MODEL), BF16)] * 2,
        compiler_params=pltpu.CompilerParams(
            dimension_semantics=("arbitrary",), vmem_limit_bytes=VMEM_LIMIT_BYTES),
        name="memory_kv",
    )(mem, g_mem, wk, wv)


def _in_proj_kernel(x_ref, g_ref, wq_ref, wk_ref, wv_ref, wog_ref, wlr_ref, wxr_ref, wxg_ref,
                    wgk_ref, bgk_ref,
                    q_ref, k_ref, v_ref, og_ref, glog_ref, xr_ref, xg_ref):
    h = _rmsnorm(x_ref[...], g_ref[...]).astype(BF16)
    q_ref[...] = _dot(h, wq_ref[...]).astype(BF16)
    k_ref[...] = _dot(h, wk_ref[...]).astype(BF16)
    v_ref[...] = _dot(h, wv_ref[...]).astype(BF16)
    og_ref[...] = _dot(h, wog_ref[...]).astype(BF16)
    xr_ref[...] = _dot(h, wxr_ref[...]).astype(BF16)
    xg_ref[...] = _dot(h, wxg_ref[...]).astype(BF16)
    lr = _dot(h, wlr_ref[...])
    z = _dot(lr.astype(BF16), wgk_ref[...]) + bgk_ref[...]
    glog_ref[...] = -_softplus(-z) * (1.0 / GLA_GATE_NORMALIZER)


def _in_proj(x2d, g_mix, wq, wk, wv, wog, wlr, wxr, wxg, wgk, bgk):
    n_tok = x2d.shape[0]
    tm = TM_IN
    const = lambda i: (0, 0)
    row = lambda i: (i, 0)
    w_spec = lambda w: pl.BlockSpec(w.shape, const)
    out_w = (GLA_DK_TOTAL, GLA_DK_TOTAL, D_GLA, D_GLA, GLA_DK_TOTAL, D_LRU, D_LRU)
    out_dt = (BF16, BF16, BF16, BF16, F32, BF16, BF16)
    return pl.pallas_call(
        _in_proj_kernel,
        grid=(n_tok // tm,),
        in_specs=[pl.BlockSpec((tm, D_MODEL), row), pl.BlockSpec((1, D_MODEL), const)]
        + [w_spec(w) for w in (wq, wk, wv, wog, wlr, wxr, wxg, wgk)]
        + [pl.BlockSpec((1, GLA_DK_TOTAL), const)],
        out_specs=[pl.BlockSpec((tm, w), row) for w in out_w],
        out_shape=[jax.ShapeDtypeStruct((n_tok, w), dt) for w, dt in zip(out_w, out_dt)],
        compiler_params=pltpu.CompilerParams(
            dimension_semantics=("arbitrary",), vmem_limit_bytes=VMEM_LIMIT_BYTES),
        name="in_proj",
    )(x2d, g_mix, wq, wk, wv, wog, wlr, wxr, wxg, wgk, bgk)


def _gla_gather_rows_back(b, p, s):
    x = b
    bit = 0
    while (1 << bit) < s:
        sh = 1 << bit
        cond = jnp.logical_and(p >= s, ((p - s) >> bit) & 1 == 1)
        x = jnp.where(cond, pltpu.roll(x, sh, 0), x)
        bit += 1
    return x


def _gla_gather_rows_fwd(b, p, s, n_rows):
    x = b
    u = s - p
    bit = 0
    while (1 << bit) <= s:
        sh = 1 << bit
        cond = jnp.logical_and(p < s, (u >> bit) & 1 == 1)
        x = jnp.where(cond, pltpu.roll(x, n_rows - sh, 0), x)
        bit += 1
    return x


def _gla_chunk(qc, kc, gc, vc, st_ref, consts):
    tril, row, lane_head, head_masks, level_masks, eseg, bd_mask = consts
    c = GLA_CHUNK
    g_hi = gc.astype(BF16)
    g_lo = (gc - g_hi.astype(F32)).astype(BF16)
    b = _dot(tril, g_hi) + _dot(tril, g_lo)
    b_last = b[c - 1:c, :]
    scale = GLA_DK ** -0.5

    st = st_ref[...]
    q_in = (qc * jnp.exp(b) * scale).astype(BF16)
    o = _dot_nt(q_in, st.astype(BF16))

    o = o + _dot((qc * kc * scale).astype(BF16), eseg) * vc.astype(F32)

    a_sc = jnp.zeros((GLA_HEADS * c, c), F32)
    for li, s in enumerate((32, 16, 8, 4, 2, 1)):
        p = row & (2 * s - 1)
        dq = b - _gla_gather_rows_back(b, p, s)
        dk = jnp.where(p < s, _gla_gather_rows_fwd(b, p, s, c) - b, 0.0)
        qt = qc * jnp.exp(dq) * scale
        kt = (kc * jnp.exp(dk)).astype(BF16)
        q_exp = jnp.concatenate(
            [jnp.where(head_masks[h], qt, 0.0) for h in range(GLA_HEADS)], axis=0).astype(BF16)
        sc = _dot_nt(q_exp, kt)
        a_sc = a_sc + jnp.where(level_masks[li], sc, 0.0)
    intra = [
        _dot(a_sc[h * c:(h + 1) * c, :].astype(BF16), vc[:, h * GLA_DV:(h + 1) * GLA_DV])
        for h in range(GLA_HEADS)
    ]
    o = o + jnp.concatenate(intra, axis=1)

    k_dec = (kc * jnp.exp(b_last - b)).astype(BF16)
    kv_t = _dot_tn(vc, k_dec)
    st_ref[...] = st * jnp.exp(b_last) + jnp.where(bd_mask, kv_t, 0.0)
    return o


def _mixers_kernel(q_ref, k_ref, v_ref, og_ref, glog_ref, xr_ref, xg_ref,
                   ghead_ref, cw_ref, cb_ref, wa_ref, ba_ref, wx_ref, bx_ref, lam_ref,
                   y_ref,
                   st_sc, o_sc, xbuf_sc, hcar_sc):
    tt = TT_MIX
    c = GLA_CHUNK
    t_idx = pl.program_id(1)

    @pl.when(t_idx == 0)
    def _():
        st_sc[...] = jnp.zeros_like(st_sc)
        xbuf_sc[0:SUBLANES, :] = jnp.zeros((SUBLANES, D_LRU), F32)
        hcar_sc[...] = jnp.zeros_like(hcar_sc)

    row = lax.broadcasted_iota(jnp.int32, (c, GLA_DK_TOTAL), 0)
    lane = lax.broadcasted_iota(jnp.int32, (c, GLA_DK_TOTAL), 1)
    lane_head = lane // GLA_DK
    head_masks = [lane_head == h for h in range(GLA_HEADS)]
    tri_r = lax.broadcasted_iota(jnp.int32, (c, c), 0)
    tri_c = lax.broadcasted_iota(jnp.int32, (c, c), 1)
    tril = (tri_r >= tri_c).astype(BF16)
    li_r = lax.broadcasted_iota(jnp.int32, (GLA_HEADS * c, c), 0) & (c - 1)
    li_c = lax.broadcasted_iota(jnp.int32, (GLA_HEADS * c, c), 1)
    level_masks = []
    for s in (32, 16, 8, 4, 2, 1):
        m2 = 2 * s
        same_blk = (li_r // m2) == (li_c // m2)
        level_masks.append(
            jnp.logical_and(same_blk, jnp.logical_and((li_r & (m2 - 1)) >= s, (li_c & (m2 - 1)) < s)))
    es_r = lax.broadcasted_iota(jnp.int32, (GLA_DK_TOTAL, D_GLA), 0) // GLA_DK
    es_c = lax.broadcasted_iota(jnp.int32, (GLA_DK_TOTAL, D_GLA), 1) // GLA_DV
    eseg = (es_r == es_c).astype(BF16)
    bd_r = lax.broadcasted_iota(jnp.int32, (D_GLA, GLA_DK_TOTAL), 0) // GLA_DV
    bd_c = lax.broadcasted_iota(jnp.int32, (D_GLA, GLA_DK_TOTAL), 1) // GLA_DK
    bd_mask = bd_r == bd_c
    consts = (tril, row, lane_head, head_masks, level_masks, eseg, bd_mask)

    def chunk_body(ci, carry):
        r0 = pl.multiple_of(ci * c, c)
        qc = q_ref[0, pl.ds(r0, c), :].astype(F32)
        kc = k_ref[0, pl.ds(r0, c), :].astype(F32)
        gc = glog_ref[0, pl.ds(r0, c), :]
        vc = v_ref[0, pl.ds(r0, c), :]
        o_sc[pl.ds(r0, c), :] = _gla_chunk(qc, kc, gc, vc, st_sc, consts)
        return carry

    lax.fori_loop(0, tt // c, chunk_body, 0)

    o = o_sc[...]
    og = og_ref[0].astype(F32)
    ghead = ghead_ref[...]
    for h in range(GLA_HEADS):
        sl = slice(h * GLA_DV, (h + 1) * GLA_DV)
        y_ref[0, :, sl] = (_rmsnorm(o[:, sl], ghead) * _silu(og[:, sl])).astype(BF16)

    xr = xr_ref[0].astype(F32)
    xbuf_sc[SUBLANES:SUBLANES + tt, :] = xr
    cw = cw_ref[...]
    xc = (cw[3:4, :] * xr
          + cw[2:3, :] * xbuf_sc[SUBLANES - 1:SUBLANES - 1 + tt, :]
          + cw[1:2, :] * xbuf_sc[SUBLANES - 2:SUBLANES - 2 + tt, :]
          + cw[0:1, :] * xbuf_sc[SUBLANES - 3:SUBLANES - 3 + tt, :]
          + cb_ref[...])
    xbuf_sc[0:SUBLANES, :] = xr[tt - SUBLANES:tt, :]

    xcb = xc.astype(BF16)
    r_gate = _sigmoid(_dot(xcb, wa_ref[...]) + ba_ref[...])
    i_gate = _sigmoid(_dot(xcb, wx_ref[...]) + bx_ref[...])
    log_a = (-LRU_C) * r_gate * _softplus(-lam_ref[...])
    a = jnp.exp(log_a)
    th = jnp.tanh(log_a)
    mult = jnp.sqrt(-2.0 * th / (1.0 - th))
    trow = lax.broadcasted_iota(jnp.int32, (tt, D_LRU), 0)
    mult = jnp.where(jnp.logical_and(trow == 0, t_idx == 0), 1.0, mult)
    bterm = mult * (i_gate * xc)

    sh = 1
    while sh < tt:
        keep = trow >= sh
        a_s = jnp.where(keep, pltpu.roll(a, sh, 0), 1.0)
        b_s = jnp.where(keep, pltpu.roll(bterm, sh, 0), 0.0)
        bterm = a * b_s + bterm
        a = a * a_s
        sh *= 2
    hseq = bterm + a * hcar_sc[0:1, :]
    hcar_sc[0:1, :] = hseq[tt - 1:tt, :]

    y_ref[0, :, D_GLA:] = (hseq * _gelu_tanh(xg_ref[0].astype(F32))).astype(BF16)


def _mixers(q, k, v, og, glog, xr, xg, ghead, cw, cb, wa_bd, ba, wx_bd, bx, lam):
    bsz, seq, _ = q.shape
    tt = TT_MIX
    tok = lambda w: pl.BlockSpec((1, tt, w), lambda b, t: (b, t, 0))
    const = lambda shp: pl.BlockSpec(shp, lambda b, t: (0, 0))
    return pl.pallas_call(
        _mixers_kernel,
        grid=(bsz, seq // tt),
        in_specs=[tok(GLA_DK_TOTAL), tok(GLA_DK_TOTAL), tok(D_GLA), tok(D_GLA), tok(GLA_DK_TOTAL),
                  tok(D_LRU), tok(D_LRU),
                  const((1, GLA_DV)), const((LRU_CONV_W, D_LRU)), const((1, D_LRU)),
                  const((D_LRU, D_LRU)), const((1, D_LRU)), const((D_LRU, D_LRU)), const((1, D_LRU)),
                  const((1, D_LRU))],
        out_specs=pl.BlockSpec((1, tt, D_MODEL), lambda b, t: (b, t, 0)),
        out_shape=jax.ShapeDtypeStruct((bsz, seq, D_MODEL), BF16),
        scratch_shapes=[
            pltpu.VMEM((D_GLA, GLA_DK_TOTAL), F32),
            pltpu.VMEM((tt, D_GLA), F32),
            pltpu.VMEM((tt + SUBLANES, D_LRU), F32),
            pltpu.VMEM((SUBLANES, D_LRU), F32),
        ],
        compiler_params=pltpu.CompilerParams(
            dimension_semantics=("arbitrary", "arbitrary"), vmem_limit_bytes=VMEM_LIMIT_BYTES),
        name="mixers",
    )(q, k, v, og, glog, xr, xg, ghead, cw, cb, wa_bd, ba, wx_bd, bx, lam)


def _tail_pre_ffn(x, y, km_ref, vm_ref, wmo_ref, gxa_ref, wxq_ref, wxo_ref):
    x1 = x + _dot(y, wmo_ref[...])
    hq = _rmsnorm(x1, gxa_ref[...]).astype(BF16)
    xq = _dot(hq, wxq_ref[...])
    att = []
    sm_scale = XATTN_HEAD_DIM ** -0.5
    for h in range(XATTN_HEADS):
        sl = slice(h * XATTN_HEAD_DIM, (h + 1) * XATTN_HEAD_DIM)
        s = _dot_nt(xq[:, sl].astype(BF16), km_ref[0, :, sl]) * sm_scale
        e = jnp.exp(s - jnp.max(s, axis=-1, keepdims=True))
        p = e * (1.0 / jnp.sum(e, axis=-1, keepdims=True))
        att.append(_dot(p.astype(BF16), vm_ref[0, :, sl]))
    att = jnp.concatenate(att, axis=1).astype(BF16)
    return x1 + _dot(att, wxo_ref[...])


def _tail_kernel(x_ref, y_ref, km_ref, vm_ref,
                 wmo_ref, gxa_ref, wxq_ref, wxo_ref,
                 gffn_ref, wff_ref, cw_ref, cb_ref, wo_ref, gfin_ref,
                 out_ref,
                 halo_sc, x2_sc, hf_sc, f_sc):
    tm = TM_TAIL
    t_idx = pl.program_id(1)

    @pl.when(t_idx == 0)
    def _():
        halo_sc[...] = jnp.zeros_like(halo_sc)

    for r in range(TAIL_ROW_GROUPS):
        rows = slice(r * (tm // TAIL_ROW_GROUPS), (r + 1) * (tm // TAIL_ROW_GROUPS))
        x2 = _tail_pre_ffn(x_ref[0, rows, :], y_ref[0, rows, :], km_ref, vm_ref,
                           wmo_ref, gxa_ref, wxq_ref, wxo_ref)
        x2_sc[rows, :] = x2
        hf_sc[rows, :] = _rmsnorm(x2, gffn_ref[...]).astype(BF16)

    for c in range(N_FF_CHUNKS):
        cols = slice(c * FF_CHUNK, (c + 1) * FF_CHUNK)
        ucols = slice(D_FF + c * FF_CHUNK, D_FF + (c + 1) * FF_CHUNK)
        gate = _dot(hf_sc[...], wff_ref[:, cols])
        up = _dot(hf_sc[...], wff_ref[:, ucols])
        gcat = jnp.concatenate([halo_sc[:, cols], gate], axis=0)
        conv = (cw_ref[2:3, cols] * gate
                + cw_ref[1:2, cols] * pltpu.roll(gcat, 1, 0)[SUBLANES:, :]
                + cw_ref[0:1, cols] * pltpu.roll(gcat, 2, 0)[SUBLANES:, :]
                + cb_ref[:, cols])
        halo_sc[:, cols] = gate[tm - SUBLANES:tm, :]
        f_sc[:, cols] = (_silu(conv) * up).astype(BF16)

    x3 = x2_sc[...] + _dot(f_sc[...], wo_ref[...])
    out_ref[0] = _rmsnorm(x3, gfin_ref[...])


def _tail(x, y_mix, k_mem, v_mem, wmo, gxa, wxq, wxo, gffn, wff, cw, cb, wo, gfin):
    bsz, seq, _ = x.shape
    tm = TM_TAIL
    tok = lambda: pl.BlockSpec((1, tm, D_MODEL), lambda b, t: (b, t, 0))
    mem = lambda: pl.BlockSpec((1, N_MEM, D_MODEL), lambda b, t: (b, 0, 0))
    single = pl.Buffered(1)
    c2 = lambda shp: pl.BlockSpec(shp, lambda b, t: (0, 0), pipeline_mode=single)
    return pl.pallas_call(
        _tail_kernel,
        grid=(bsz, seq // tm),
        in_specs=[tok(), tok(), mem(), mem(),
                  c2((D_MODEL, D_MODEL)), c2((1, D_MODEL)), c2((D_MODEL, D_MODEL)), c2((D_MODEL, D_MODEL)),
                  c2((1, D_MODEL)),
                  c2((D_MODEL, 2 * D_FF)), c2((FFN_CONV_W, D_FF)), c2((1, D_FF)), c2((D_FF, D_MODEL)),
                  c2((1, D_MODEL))],
        out_specs=tok(),
        out_shape=jax.ShapeDtypeStruct((bsz, seq, D_MODEL), F32),
        scratch_shapes=[
            pltpu.VMEM((SUBLANES, D_FF), F32),
            pltpu.VMEM((tm, D_MODEL), F32),
            pltpu.VMEM((tm, D_MODEL), BF16),
            pltpu.VMEM((tm, D_FF), BF16),
        ],
        compiler_params=pltpu.CompilerParams(
            dimension_semantics=("arbitrary", "arbitrary"), vmem_limit_bytes=VMEM_LIMIT_BYTES),
        name="tail",
    )(x, y_mix, k_mem, v_mem, wmo, gxa, wxq, wxo, gffn, wff, cw, cb, wo, gfin)


def _block_diag(w):
    h, n, _ = w.shape
    eye = jnp.eye(h, dtype=w.dtype)
    return (eye[:, None, :, None] * w[:, :, None, :]).reshape(h * n, h * n)


def kernel(x, mem, g_mix, w_in, w_gk, b_gk, g_gla_head, conv_lru_w, conv_lru_b, w_rg_a, b_rg_a, w_rg_x, b_rg_x, lru_lambda, w_mix_out, g_xattn, g_mem, w_xq, w_xkv, w_xo, g_ffn, w_ffn_in, conv_ffn_w, conv_ffn_b, w_ffn_out, g_final):
    bsz, seq, d = x.shape
    l = 0

    wi = w_in[l].astype(BF16)
    o_q, o_k, o_v, o_og, o_lr, o_xr, o_xg = 0, 256, 512, 1024, 1536, 1552, 2064
    wq, wk, wv, wog = wi[:, o_q:o_k], wi[:, o_k:o_v], wi[:, o_v:o_og], wi[:, o_og:o_lr]
    wlr = jnp.pad(wi[:, o_lr:o_xr], ((0, 0), (0, LANES - GLA_LOWRANK)))
    wxr, wxg = wi[:, o_xr:o_xg], wi[:, o_xg:]
    wgk = jnp.pad(w_gk[l].astype(BF16), ((0, LANES - GLA_LOWRANK), (0, 0)))
    row2 = lambda v: v.reshape(1, -1).astype(F32)

    wkv = w_xkv[l].astype(BF16)
    k_mem, v_mem = _memory_kv(mem, row2(g_mem[l]), wkv[:, :D_MODEL], wkv[:, D_MODEL:])

    q, k, v, og, glog, xr, xg = _in_proj(
        x.reshape(bsz * seq, d), row2(g_mix[l]), wq, wk, wv, wog, wlr, wxr, wxg, wgk, row2(b_gk[l]))
    r3 = lambda t: t.reshape(bsz, seq, t.shape[-1])

    y_mix = _mixers(
        r3(q), r3(k), r3(v), r3(og), r3(glog), r3(xr), r3(xg),
        row2(g_gla_head[l]), conv_lru_w[l].astype(F32), row2(conv_lru_b[l]),
        _block_diag(w_rg_a[l]).astype(BF16), row2(b_rg_a[l]),
        _block_diag(w_rg_x[l]).astype(BF16), row2(b_rg_x[l]), row2(lru_lambda[l]))

    return _tail(
        x, y_mix, k_mem, v_mem,
        w_mix_out[l].astype(BF16), row2(g_xattn[l]), w_xq[l].astype(BF16), w_xo[l].astype(BF16),
        row2(g_ffn[l]), w_ffn_in[l].astype(BF16), conv_ffn_w[l].astype(F32), row2(conv_ffn_b[l]),
        w_ffn_out[l].astype(BF16), row2(g_final))
```

```python
import functools
import math

import jax
import jax.numpy as jnp
from jax import lax
from jax.experimental import pallas as pl
from jax.experimental.pallas import tpu as pltpu

F32 = jnp.float32
BF16 = jnp.bfloat16

D_MODEL = 1024
N_MEM = 256
XATTN_HEADS = 4
XATTN_HEAD_DIM = D_MODEL // XATTN_HEADS
D_GLA = 512
D_LRU = 512
GLA_HEADS = 4
GLA_DK_TOTAL = 256
GLA_DK = 64
GLA_DV = 128
GLA_LOWRANK = 16
GLA_GATE_NORMALIZER = 16.0
GLA_CHUNK = 64
LRU_CONV_W = 4
LRU_C = 8.0
D_FF = 2816
FFN_CONV_W = 3
EPS = 1e-6

LANES = 128
SUBLANES = 8
MXU_TILE = 256
VMEM_LIMIT_BYTES = 56 * 1024 * 1024

TT_MIX = 512
TM_TAIL = 512
TAIL_ROW_GROUPS = 2
FF_CHUNK = 256
N_FF_CHUNKS = D_FF // FF_CHUNK
assert N_FF_CHUNKS * FF_CHUNK == D_FF

GLA_BIG_LEVELS = (32, 16, 8)
GLA_SMALL_LEVELS = (4, 2, 1)
GLA_GROUPS = GLA_CHUNK // SUBLANES


def _rmsnorm(x, g):
    return x * lax.rsqrt(jnp.mean(x * x, axis=-1, keepdims=True) + EPS) * g


def _softplus(x):
    return jnp.maximum(x, 0.0) + jnp.log1p(jnp.exp(-jnp.abs(x)))


def _sigmoid(x):
    return 1.0 / (1.0 + jnp.exp(-x))


def _silu(x):
    return x * _sigmoid(x)


def _gelu_tanh(x):
    c = math.sqrt(2.0 / math.pi)
    return 0.5 * x * (1.0 + jnp.tanh(c * (x + 0.044715 * (x * x * x))))


def _dot(a, b):
    return jnp.dot(a, b, preferred_element_type=F32)


def _dot_nt(a, b):
    return lax.dot_general(a, b, (((1,), (1,)), ((), ())), preferred_element_type=F32)


def _dot_tn(a, b):
    return lax.dot_general(a, b, (((0,), (0,)), ((), ())), preferred_element_type=F32)


def _memory_kv_kernel(mem_ref, g_ref, wk_ref, wv_ref, k_ref, v_ref):
    hm = _rmsnorm(mem_ref[0], g_ref[...]).astype(BF16)
    k_ref[0] = _dot(hm, wk_ref[...]).astype(BF16)
    v_ref[0] = _dot(hm, wv_ref[...]).astype(BF16)


def _memory_kv(mem, g_mem, wk, wv):
    bsz = mem.shape[0]
    const = lambda b: (0, 0)
    return pl.pallas_call(
        _memory_kv_kernel,
        grid=(bsz,),
        in_specs=[
            pl.BlockSpec((1, N_MEM, D_MODEL), lambda b: (b, 0, 0)),
            pl.BlockSpec((1, D_MODEL), const),
            pl.BlockSpec((D_MODEL, D_MODEL), const),
            pl.BlockSpec((D_MODEL, D_MODEL), const),
        ],
        out_specs=[
            pl.BlockSpec((1, N_MEM, D_MODEL), lambda b: (b, 0, 0)),
            pl.BlockSpec((1, N_MEM, D_MODEL), lambda b: (b, 0, 0)),
        ],
        out_shape=[jax.ShapeDtypeStruct((bsz, N_MEM, D_MODEL), BF16)] * 2,
        compiler_params=pltpu.CompilerParams(
            dimension_semantics=("arbitrary",), vmem_limit_bytes=VMEM_LIMIT_BYTES),
        name="memory_kv",
    )(mem, g_mem, wk, wv)


def _gather_rows_back(b, p, s):
    x = b
    bit = 0
    while (1 << bit) < s:
        cond = jnp.logical_and(p >= s, ((p - s) >> bit) & 1 == 1)
        x = jnp.where(cond, pltpu.roll(x, 1 << bit, 0), x)
        bit += 1
    return x


def _gather_rows_fwd(b, p, s, n_rows):
    x = b
    u = s - p
    bit = 0
    while (1 << bit) <= s:
        cond = jnp.logical_and(p < s, (u >> bit) & 1 == 1)
        x = jnp.where(cond, pltpu.roll(x, n_rows - (1 << bit), 0), x)
        bit += 1
    return x


def _gla_consts():
    c = GLA_CHUNK
    row = lax.broadcasted_iota(jnp.int32, (c, GLA_DK_TOTAL), 0)
    lane_head = lax.broadcasted_iota(jnp.int32, (c, GLA_DK_TOTAL), 1) // GLA_DK
    head_masks = [lane_head == h for h in range(GLA_HEADS)]
    half_lane_head = lax.broadcasted_iota(jnp.int32, (c // 2, GLA_DK_TOTAL), 1) // GLA_DK
    half_head_masks = [half_lane_head == h for h in range(GLA_HEADS)]
    tri_r = lax.broadcasted_iota(jnp.int32, (c, c), 0)
    tri_c = lax.broadcasted_iota(jnp.int32, (c, c), 1)
    tril = (tri_r >= tri_c).astype(BF16)
    big_masks = {}
    half = c // 2
    br = lax.broadcasted_iota(jnp.int32, (GLA_HEADS * half, c), 0) & (half - 1)
    bc = lax.broadcasted_iota(jnp.int32, (GLA_HEADS * half, c), 1)
    for s in GLA_BIG_LEVELS:
        if 2 * s < c:
            big_masks[s] = (br // s) == (bc // (2 * s))
    sr = lax.broadcasted_iota(jnp.int32, (SUBLANES, c), 0)
    sc = lax.broadcasted_iota(jnp.int32, (SUBLANES, c), 1)
    small_masks = {}
    for s in GLA_SMALL_LEVELS:
        m2 = 2 * s
        pat = jnp.logical_and(
            jnp.logical_and((sr & (m2 - 1)) >= s, (sc & (m2 - 1)) < s),
            (sr // m2) == ((sc & (SUBLANES - 1)) // m2))
        small_masks[s] = [jnp.logical_and(pat, (sc // SUBLANES) == g) for g in range(GLA_GROUPS)]
    es_r = lax.broadcasted_iota(jnp.int32, (GLA_DK_TOTAL, D_GLA), 0) // GLA_DK
    es_c = lax.broadcasted_iota(jnp.int32, (GLA_DK_TOTAL, D_GLA), 1) // GLA_DV
    eseg = (es_r == es_c).astype(BF16)
    q_r = lax.broadcasted_iota(jnp.int32, (2 * GLA_DV, 2 * GLA_DK), 0) // GLA_DV
    q_c = lax.broadcasted_iota(jnp.int32, (2 * GLA_DV, 2 * GLA_DK), 1) // GLA_DK
    pair_mask = q_r == q_c
    return row, head_masks, half_head_masks, tril, big_masks, small_masks, eseg, pair_mask


def _gla_chunk(qc, kc, gc, vc, st_refs, consts):
    row, head_masks, half_head_masks, tril, big_masks, small_masks, eseg, pair_mask = consts
    c = GLA_CHUNK
    half = c // 2
    scale = GLA_DK ** -0.5
    pair_w = 2 * GLA_DK
    pair_v = 2 * GLA_DV

    g_hi = gc.astype(BF16)
    g_lo = (gc - g_hi.astype(F32)).astype(BF16)
    b = _dot(tril, g_hi) + _dot(tril, g_lo)
    b_last = b[c - 1:c, :]

    q_in = (qc * jnp.exp(b) * scale).astype(BF16)
    st = [st_refs[i][...] for i in range(2)]
    o = jnp.concatenate(
        [_dot_nt(q_in[:, i * pair_w:(i + 1) * pair_w], st[i].astype(BF16)) for i in range(2)], axis=1)

    o = o + _dot((qc * kc * scale).astype(BF16), eseg) * vc.astype(F32)

    acc = [[None] * GLA_GROUPS for _ in range(GLA_HEADS)]

    def add_piece(h, g, val):
        acc[h][g] = val if acc[h][g] is None else acc[h][g] + val

    for s in GLA_BIG_LEVELS:
        nb = c // (2 * s)
        q_parts, k_parts = [], []
        for blk in range(nb):
            base = blk * 2 * s
            anchor = b[base + s:base + s + 1, :]
            dq = b[base + s:base + 2 * s, :] - anchor
            dk = anchor - b[base:base + s, :]
            q_parts.append(qc[base + s:base + 2 * s, :] * jnp.exp(dq) * scale)
            k_parts.append((kc[base:base + s, :] * jnp.exp(dk)).astype(BF16))
            k_parts.append(jnp.zeros((s, GLA_DK_TOTAL), BF16))
        qt = jnp.concatenate(q_parts, axis=0)
        kt = jnp.concatenate(k_parts, axis=0)
        q_exp = jnp.concatenate(
            [jnp.where(half_head_masks[h], qt, 0.0) for h in range(GLA_HEADS)], axis=0).astype(BF16)
        sc = _dot_nt(q_exp, kt)
        if nb > 1:
            sc = jnp.where(big_masks[s], sc, 0.0)
        for h in range(GLA_HEADS):
            for blk in range(nb):
                for r8 in range(s // SUBLANES):
                    src = h * half + blk * s + r8 * SUBLANES
                    add_piece(h, (blk * 2 * s + s) // SUBLANES + r8, sc[src:src + SUBLANES, :])

    for s in GLA_SMALL_LEVELS:
        p = row & (2 * s - 1)
        dq = b - _gather_rows_back(b, p, s)
        dk = jnp.where(p < s, _gather_rows_fwd(b, p, s, c) - b, 0.0)
        qt = qc * jnp.exp(dq) * scale
        kt = (kc * jnp.exp(dk)).astype(BF16)
        q_exp = jnp.concatenate(
            [jnp.where(head_masks[h], qt, 0.0) for h in range(GLA_HEADS)], axis=0).astype(BF16)
        sc = _dot_nt(q_exp, kt)
        for h in range(GLA_HEADS):
            for g in range(GLA_GROUPS):
                src = h * c + g * SUBLANES
                add_piece(h, g, jnp.where(small_masks[s][g], sc[src:src + SUBLANES, :], 0.0))

    intra = []
    for h in range(GLA_HEADS):
        a_h = jnp.concatenate(acc[h], axis=0).astype(BF16)
        intra.append(_dot(a_h, vc[:, h * GLA_DV:(h + 1) * GLA_DV]))
    o = o + jnp.concatenate(intra, axis=1)

    k_dec = (kc * jnp.exp(b_last - b)).astype(BF16)
    decay = jnp.exp(b_last)
    for i in range(2):
        kv_t = _dot_tn(vc[:, i * pair_v:(i + 1) * pair_v], k_dec[:, i * pair_w:(i + 1) * pair_w])
        st_refs[i][...] = (st[i] * decay[:, i * pair_w:(i + 1) * pair_w]
                           + jnp.where(pair_mask, kv_t, 0.0))
    return o


def _group_rows(x):
    return x.reshape(x.shape[0] // SUBLANES, SUBLANES, x.shape[1])


def _shift_rows_down(x_hist, k, row8):
    rot = pltpu.roll(x_hist, k, 1)
    return jnp.where(row8 >= k, rot[1:], rot[:-1])


def _lru_scan(a, bt, carry):
    n, ch = a.shape
    a = _group_rows(a)
    bt = _group_rows(bt)
    row8 = lax.broadcasted_iota(jnp.int32, a.shape, 1)
    sh = 1
    while sh < SUBLANES:
        keep = row8 >= sh
        a_s = jnp.where(keep, pltpu.roll(a, sh, 1), 1.0)
        b_s = jnp.where(keep, pltpu.roll(bt, sh, 1), 0.0)
        bt = a * b_s + bt
        a = a * a_s
        sh *= 2
    out = []
    for g in range(n // SUBLANES):
        hg = bt[g] + a[g] * carry
        out.append(hg)
        carry = hg[SUBLANES - 1:SUBLANES, :]
    return jnp.concatenate(out, axis=0), carry


def _mix_project_pieces(x_ref, g_ref, wq_ref, wk_ref, wv_ref, wog_ref, wlr_ref, wxr_ref, wxg_ref,
                        wgk_ref, bgk_ref, h_sc, bufs, slot):
    q_sc, k_sc, gl_sc, v_sc, og_sc, xr_sc, xg_sc = bufs

    def norm():
        h_sc[...] = _rmsnorm(x_ref[0], g_ref[...]).astype(BF16)

    def col_piece(dst_sc, w_ref, j, dtype):
        cols = slice(j * MXU_TILE, (j + 1) * MXU_TILE)
        def run():
            dst_sc[slot, :, cols] = _dot(h_sc[...], w_ref[:, cols]).astype(dtype)
        return run

    def decay_piece():
        lr = _dot(h_sc[...], wlr_ref[...])
        z = _dot(lr.astype(BF16), wgk_ref[...]) + bgk_ref[...]
        gl_sc[slot] = -_softplus(-z) * (1.0 / GLA_GATE_NORMALIZER)

    pieces = [col_piece(q_sc, wq_ref, 0, F32), col_piece(k_sc, wk_ref, 0, F32), decay_piece]
    for dst_sc, w_ref, dtype in ((v_sc, wv_ref, BF16), (og_sc, wog_ref, F32),
                                 (xr_sc, wxr_ref, F32), (xg_sc, wxg_ref, F32)):
        pieces += [col_piece(dst_sc, w_ref, j, dtype) for j in range(D_GLA // MXU_TILE)]
    return norm, pieces


def _lru_chunk(xr_c, xg_c, x_prev, carry, seq_start, cw_ref, cb_ref, wa_ref, ba_ref, wx_ref, bx_ref, lam_ref):
    n = xr_c.shape[0]
    x_hist = _group_rows(jnp.concatenate([x_prev, xr_c], axis=0))
    row8 = lax.broadcasted_iota(jnp.int32, (n // SUBLANES, SUBLANES, D_LRU), 1)
    xc = cw_ref[3:4, :] * x_hist[1:] + cb_ref[...]
    for k in range(1, LRU_CONV_W):
        xc = xc + cw_ref[LRU_CONV_W - 1 - k:LRU_CONV_W - k, :] * _shift_rows_down(x_hist, k, row8)
    xc = xc.reshape(n, D_LRU)

    xcb = xc.astype(BF16)
    gate_pre = lambda w_ref: jnp.concatenate(
        [_dot(xcb[:, i * MXU_TILE:(i + 1) * MXU_TILE],
              w_ref[i * MXU_TILE:(i + 1) * MXU_TILE, i * MXU_TILE:(i + 1) * MXU_TILE])
         for i in range(D_LRU // MXU_TILE)], axis=1)
    r_gate = _sigmoid(gate_pre(wa_ref) + ba_ref[...])
    i_gate = _sigmoid(gate_pre(wx_ref) + bx_ref[...])
    log_a = (-LRU_C) * r_gate * _softplus(-lam_ref[...])
    a = jnp.exp(log_a)
    th = jnp.tanh(log_a)
    mult = jnp.sqrt(-2.0 * th / (1.0 - th))
    if seq_start is not None:
        trow = lax.broadcasted_iota(jnp.int32, (n, D_LRU), 0)
        mult = jnp.where(jnp.logical_and(trow == 0, seq_start), 1.0, mult)
    hseq, carry = _lru_scan(a, mult * (i_gate * xc), carry)
    return hseq * _gelu_tanh(xg_c), xr_c[n - SUBLANES:n, :], carry


def _mix_step(proj, bufs, slot, seq_start, ghead_ref, lru_refs, y_ref, st0_sc, st1_sc, xhalo_sc, hcar_sc):
    q_sc, k_sc, gl_sc, v_sc, og_sc, xr_sc, xg_sc = bufs
    norm, pieces = proj
    tt = TT_MIX
    c = GLA_CHUNK
    n_chunks = tt // c

    @pl.when(seq_start)
    def _():
        st0_sc[...] = jnp.zeros_like(st0_sc)
        st1_sc[...] = jnp.zeros_like(st1_sc)
        xhalo_sc[...] = jnp.zeros_like(xhalo_sc)
        hcar_sc[...] = jnp.zeros_like(hcar_sc)

    norm()
    consts = _gla_consts()
    ghead = ghead_ref[...]
    x_prev = xhalo_sc[...]
    carry = hcar_sc[0:1, :]
    for ci in range(n_chunks):
        rows = slice(ci * c, (ci + 1) * c)
        for piece in pieces[ci * len(pieces) // n_chunks:(ci + 1) * len(pieces) // n_chunks]:
            piece()
        o = _gla_chunk(q_sc[slot, rows, :], k_sc[slot, rows, :], gl_sc[slot, rows, :], v_sc[slot, rows, :],
                       (st0_sc, st1_sc), consts)
        og = og_sc[slot, rows, :]
        for h in range(GLA_HEADS):
            sl = slice(h * GLA_DV, (h + 1) * GLA_DV)
            y_ref[0, rows, sl] = (_rmsnorm(o[:, sl], ghead) * _silu(og[:, sl])).astype(BF16)
        y_lru, x_prev, carry = _lru_chunk(
            xr_sc[slot, rows, :], xg_sc[slot, rows, :], x_prev, carry,
            seq_start if ci == 0 else None, *lru_refs)
        y_ref[0, rows, D_GLA:] = y_lru.astype(BF16)
    xhalo_sc[...] = x_prev
    hcar_sc[0:1, :] = carry


def _mix_kernel(tiles_per_seq,
                x_ref, g_ref, wq_ref, wk_ref, wv_ref, wog_ref, wlr_ref, wxr_ref, wxg_ref,
                wgk_ref, bgk_ref, ghead_ref, cw_ref, cb_ref, wa_ref, ba_ref, wx_ref, bx_ref, lam_ref,
                y_ref,
                h_sc, q_sc, k_sc, gl_sc, v_sc, og_sc, xr_sc, xg_sc, st0_sc, st1_sc, xhalo_sc, hcar_sc):
    t = pl.program_id(0)
    bufs = (q_sc, k_sc, gl_sc, v_sc, og_sc, xr_sc, xg_sc)
    proj_refs = (x_ref, g_ref, wq_ref, wk_ref, wv_ref, wog_ref, wlr_ref, wxr_ref, wxg_ref, wgk_ref, bgk_ref)
    lru_refs = (cw_ref, cb_ref, wa_ref, ba_ref, wx_ref, bx_ref, lam_ref)
    seq_start = lax.rem(t - 1, tiles_per_seq) == 0

    @pl.when(t == 0)
    def _():
        norm, pieces = _mix_project_pieces(*proj_refs, h_sc, bufs, 0)
        norm()
        for piece in pieces:
            piece()

    for parity in range(2):
        @pl.when(jnp.logical_and(t > 0, lax.rem(t, 2) == parity))
        def _():
            _mix_step(_mix_project_pieces(*proj_refs, h_sc, bufs, parity), bufs, 1 - parity, seq_start,
                      ghead_ref, lru_refs, y_ref, st0_sc, st1_sc, xhalo_sc, hcar_sc)


def _mix(x, g_mix, wq, wk, wv, wog, wlr, wxr, wxg, wgk, bgk, ghead, cw, cb, wa_bd, ba, wx_bd, bx, lam):
    bsz, seq, _ = x.shape
    tt = TT_MIX
    tiles_per_seq = seq // tt
    n_tiles = bsz * tiles_per_seq
    single = pl.Buffered(1)
    const = lambda a: pl.BlockSpec(a.shape, lambda t: (0, 0), pipeline_mode=single)
    params = (g_mix, wq, wk, wv, wog, wlr, wxr, wxg, wgk, bgk, ghead, cw, cb, wa_bd, ba, wx_bd, bx, lam)
    two = lambda w, dt: pltpu.VMEM((2, tt, w), dt)
    y = pl.pallas_call(
        functools.partial(_mix_kernel, tiles_per_seq),
        grid=(n_tiles + 1,),
        in_specs=[pl.BlockSpec((1, tt, D_MODEL), lambda t: (jnp.minimum(t, n_tiles - 1), 0, 0))]
        + [const(p) for p in params],
        out_specs=pl.BlockSpec((1, tt, D_MODEL), lambda t: (jnp.maximum(t - 1, 0), 0, 0)),
        out_shape=jax.ShapeDtypeStruct((n_tiles, tt, D_MODEL), BF16),
        scratch_shapes=[
            pltpu.VMEM((tt, D_MODEL), BF16),
            two(GLA_DK_TOTAL, F32),
            two(GLA_DK_TOTAL, F32),
            two(GLA_DK_TOTAL, F32),
            two(D_GLA, BF16),
            two(D_GLA, F32),
            two(D_LRU, F32),
            two(D_LRU, F32),
            pltpu.VMEM((2 * GLA_DV, 2 * GLA_DK), F32),
            pltpu.VMEM((2 * GLA_DV, 2 * GLA_DK), F32),
            pltpu.VMEM((SUBLANES, D_LRU), F32),
            pltpu.VMEM((SUBLANES, D_LRU), F32),
        ],
        compiler_params=pltpu.CompilerParams(
            dimension_semantics=("arbitrary",), vmem_limit_bytes=VMEM_LIMIT_BYTES),
        name="mix",
    )(x.reshape(n_tiles, tt, D_MODEL), *params)
    return y.reshape(bsz, seq, D_MODEL)


def _tail_pre_ffn(x, y, km_ref, vm_ref, wmo_ref, gxa_ref, wxq_ref, wxo_ref):
    x1 = x + _dot(y, wmo_ref[...])
    hq = _rmsnorm(x1, gxa_ref[...]).astype(BF16)
    xq = _dot(hq, wxq_ref[...])
    att = []
    sm_scale = XATTN_HEAD_DIM ** -0.5
    for h in range(XATTN_HEADS):
        sl = slice(h * XATTN_HEAD_DIM, (h + 1) * XATTN_HEAD_DIM)
        s = _dot_nt(xq[:, sl].astype(BF16), km_ref[0, :, sl]) * sm_scale
        e = jnp.exp(s - jnp.max(s, axis=-1, keepdims=True))
        p = e * (1.0 / jnp.sum(e, axis=-1, keepdims=True))
        att.append(_dot(p.astype(BF16), vm_ref[0, :, sl]))
    att = jnp.concatenate(att, axis=1).astype(BF16)
    return x1 + _dot(att, wxo_ref[...])


def _tail_kernel(x_ref, y_ref, km_ref, vm_ref,
                 wmo_ref, gxa_ref, wxq_ref, wxo_ref,
                 gffn_ref, wff_ref, cw_ref, cb_ref, wo_ref, gfin_ref,
                 out_ref,
                 halo_sc, x2_sc, hf_sc, f_sc):
    tm = TM_TAIL
    t_idx = pl.program_id(1)

    @pl.when(t_idx == 0)
    def _():
        halo_sc[...] = jnp.zeros_like(halo_sc)

    for r in range(TAIL_ROW_GROUPS):
        rows = slice(r * (tm // TAIL_ROW_GROUPS), (r + 1) * (tm // TAIL_ROW_GROUPS))
        x2 = _tail_pre_ffn(x_ref[0, rows, :], y_ref[0, rows, :], km_ref, vm_ref,
                           wmo_ref, gxa_ref, wxq_ref, wxo_ref)
        x2_sc[rows, :] = x2
        hf_sc[rows, :] = _rmsnorm(x2, gffn_ref[...]).astype(BF16)

    for c in range(N_FF_CHUNKS):
        cols = slice(c * FF_CHUNK, (c + 1) * FF_CHUNK)
        ucols = slice(D_FF + c * FF_CHUNK, D_FF + (c + 1) * FF_CHUNK)
        gate = _dot(hf_sc[...], wff_ref[:, cols])
        up = _dot(hf_sc[...], wff_ref[:, ucols])
        gcat = jnp.concatenate([halo_sc[:, cols], gate], axis=0)
        conv = (cw_ref[2:3, cols] * gate
                + cw_ref[1:2, cols] * pltpu.roll(gcat, 1, 0)[SUBLANES:, :]
                + cw_ref[0:1, cols] * pltpu.roll(gcat, 2, 0)[SUBLANES:, :]
                + cb_ref[:, cols])
        halo_sc[:, cols] = gate[tm - SUBLANES:tm, :]
        f_sc[:, cols] = (_silu(conv) * up).astype(BF16)

    x3 = x2_sc[...] + _dot(f_sc[...], wo_ref[...])
    out_ref[0] = _rmsnorm(x3, gfin_ref[...])


def _tail(x, y_mix, k_mem, v_mem, wmo, gxa, wxq, wxo, gffn, wff, cw, cb, wo, gfin):
    bsz, seq, _ = x.shape
    tm = TM_TAIL
    tok = lambda: pl.BlockSpec((1, tm, D_MODEL), lambda b, t: (b, t, 0))
    mem = lambda: pl.BlockSpec((1, N_MEM, D_MODEL), lambda b, t: (b, 0, 0))
    single = pl.Buffered(1)
    c2 = lambda shp: pl.BlockSpec(shp, lambda b, t: (0, 0), pipeline_mode=single)
    return pl.pallas_call(
        _tail_kernel,
        grid=(bsz, seq // tm),
        in_specs=[tok(), tok(), mem(), mem(),
                  c2((D_MODEL, D_MODEL)), c2((1, D_MODEL)), c2((D_MODEL, D_MODEL)), c2((D_MODEL, D_MODEL)),
                  c2((1, D_MODEL)),
                  c2((D_MODEL, 2 * D_FF)), c2((FFN_CONV_W, D_FF)), c2((1, D_FF)), c2((D_FF, D_MODEL)),
                  c2((1, D_MODEL))],
        out_specs=tok(),
        out_shape=jax.ShapeDtypeStruct((bsz, seq, D_MODEL), F32),
        scratch_shapes=[
            pltpu.VMEM((SUBLANES, D_FF), F32),
            pltpu.VMEM((tm, D_MODEL), F32),
            pltpu.VMEM((tm, D_MODEL), BF16),
            pltpu.VMEM((tm, D_FF), BF16),
        ],
        compiler_params=pltpu.CompilerParams(
            dimension_semantics=("arbitrary", "arbitrary"), vmem_limit_bytes=VMEM_LIMIT_BYTES),
        name="tail",
    )(x, y_mix, k_mem, v_mem, wmo, gxa, wxq, wxo, gffn, wff, cw, cb, wo, gfin)


def _block_diag(w):
    h, n, _ = w.shape
    eye = jnp.eye(h, dtype=w.dtype)
    return (eye[:, None, :, None] * w[:, :, None, :]).reshape(h * n, h * n)


def kernel(x, mem, g_mix, w_in, w_gk, b_gk, g_gla_head, conv_lru_w, conv_lru_b, w_rg_a, b_rg_a, w_rg_x, b_rg_x, lru_lambda, w_mix_out, g_xattn, g_mem, w_xq, w_xkv, w_xo, g_ffn, w_ffn_in, conv_ffn_w, conv_ffn_b, w_ffn_out, g_final):
    l = 0

    wi = w_in[l].astype(BF16)
    o_q, o_k, o_v, o_og, o_lr, o_xr, o_xg = 0, 256, 512, 1024, 1536, 1552, 2064
    wq, wk, wv, wog = wi[:, o_q:o_k], wi[:, o_k:o_v], wi[:, o_v:o_og], wi[:, o_og:o_lr]
    wlr = jnp.pad(wi[:, o_lr:o_xr], ((0, 0), (0, LANES - GLA_LOWRANK)))
    wxr, wxg = wi[:, o_xr:o_xg], wi[:, o_xg:]
    wgk = jnp.pad(w_gk[l].astype(BF16), ((0, LANES - GLA_LOWRANK), (0, 0)))
    row2 = lambda v: v.reshape(1, -1).astype(F32)

    wkv = w_xkv[l].astype(BF16)
    k_mem, v_mem = _memory_kv(mem, row2(g_mem[l]), wkv[:, :D_MODEL], wkv[:, D_MODEL:])

    y_mix = _mix(
        x, row2(g_mix[l]), wq, wk, wv, wog, wlr, wxr, wxg, wgk, row2(b_gk[l]),
        row2(g_gla_head[l]), conv_lru_w[l].astype(F32), row2(conv_lru_b[l]),
        _block_diag(w_rg_a[l]).astype(BF16), row2(b_rg_a[l]),
        _block_diag(w_rg_x[l]).astype(BF16), row2(b_rg_x[l]), row2(lru_lambda[l]))

    return _tail(
        x, y_mix, k_mem, v_mem,
        w_mix_out[l].astype(BF16), row2(g_xattn[l]), w_xq[l].astype(BF16), w_xo[l].astype(BF16),
        row2(g_ffn[l]), w_ffn_in[l].astype(BF16), conv_ffn_w[l].astype(F32), row2(conv_ffn_b[l]),
        w_ffn_out[l].astype(BF16), row2(g_final))
```
